```python
import math
import jax, jax.numpy as jnp
from jax import lax
import numpy as np

D_MODEL = 1024
BATCH = 8
SEQ = 4096
DEPTH = 4

D_MIX = D_MODEL
D_FF = 2816
NORM_EPS = 1e-6
CONV_W = 4
A_WIDTH = 384
A_HEADS = 6
A_HEAD_DIM = A_WIDTH // A_HEADS
LRU_C = 8.0
B_WIDTH = 384
B_HEADS = 6
B_HEAD_DIM = B_WIDTH // B_HEADS
B_GROUPS = 2
B_STATE = 128
B_CHUNK = 128
B_CONV_DIM = B_WIDTH + 2 * B_GROUPS * B_STATE
B_IN = B_WIDTH + B_CONV_DIM + B_HEADS
C_WIDTH = 256
C_GROUPS = 4
C_GROUP_DIM = C_WIDTH // C_GROUPS
C_CHUNK = 128
IN_COLS = 2 * A_WIDTH + B_IN + 2 * C_WIDTH

kernel_name = "hybrid_lru_ssd_gmlp_macaron"


def rms_norm(x, g):
    xf = x.astype(jnp.float32)
    y = xf * lax.rsqrt(jnp.mean(xf * xf, axis=-1, keepdims=True) + NORM_EPS)
    return (y * g.astype(jnp.float32)).astype(x.dtype)


def layer_norm(x, g, b):
    xf = x.astype(jnp.float32)
    mu = jnp.mean(xf, axis=-1, keepdims=True)
    xc = xf - mu
    y = xc * lax.rsqrt(jnp.mean(xc * xc, axis=-1, keepdims=True) + NORM_EPS)
    return (y * g.astype(jnp.float32) + b.astype(jnp.float32)).astype(x.dtype)


def swiglu(x, w_gu, w_down):
    g, u = jnp.split(x @ w_gu, 2, axis=-1)
    return (jax.nn.silu(g) * u) @ w_down


def causal_dwconv(x, w, b):
    c = x.shape[-1]
    y = lax.conv_general_dilated(
        x, w[:, None, :].astype(x.dtype), window_strides=(1,),
        padding=((CONV_W - 1, 0),), dimension_numbers=("NWC", "WIO", "NWC"),
        feature_group_count=c)
    return y + b


def rg_lru(x, w_r, b_r, w_i, b_i, lam):
    bsz, s, _ = x.shape
    xh = x.reshape(bsz, s, A_HEADS, A_HEAD_DIM)
    r = jax.nn.sigmoid(jnp.einsum("bshi,hij->bshj", xh, w_r).reshape(bsz, s, A_WIDTH) + b_r)
    i = jax.nn.sigmoid(jnp.einsum("bshi,hij->bshj", xh, w_i).reshape(bsz, s, A_WIDTH) + b_i)
    log_a = -LRU_C * r.astype(jnp.float32) * jax.nn.softplus(-lam.astype(jnp.float32))
    a = jnp.exp(log_a)
    mult = jnp.sqrt(-jnp.expm1(2.0 * log_a))
    u = mult * (i * x).astype(jnp.float32)

    def combine(left, right):
        a1, b1 = left
        a2, b2 = right
        return a1 * a2, a2 * b1 + b2

    _, h = lax.associative_scan(combine, (a, u), axis=1)
    return h.astype(x.dtype)


def segsum(a):
    t = a.shape[-1]
    cs = jnp.cumsum(a, axis=-1)
    diff = cs[..., :, None] - cs[..., None, :]
    mask = jnp.tril(jnp.ones((t, t), dtype=bool))
    return jnp.where(mask, diff, -jnp.inf)


def ssd_mixer(zxbcdt, conv_w, conv_b, dt_bias, a_log, d_skip, norm_g):
    bsz, s, _ = zxbcdt.shape
    f32 = jnp.float32
    nc = s // B_CHUNK
    hpg = B_HEADS // B_GROUPS
    z, xbc, dt = jnp.split(zxbcdt, [B_WIDTH, B_WIDTH + B_CONV_DIM], axis=-1)
    xbc = jax.nn.silu(causal_dwconv(xbc, conv_w, conv_b))
    xs, bm, cm = jnp.split(xbc, [B_WIDTH, B_WIDTH + B_GROUPS * B_STATE], axis=-1)
    dt = jax.nn.softplus(dt.astype(f32) + dt_bias.astype(f32))
    a = -jnp.exp(a_log.astype(f32))
    x_c = xs.astype(f32).reshape(bsz, nc, B_CHUNK, B_GROUPS, hpg, B_HEAD_DIM)
    dt_c = dt.reshape(bsz, nc, B_CHUNK, B_GROUPS, hpg)
    xdt = x_c * dt_c[..., None]
    ad = jnp.moveaxis((dt * a).reshape(bsz, nc, B_CHUNK, B_GROUPS, hpg), 2, -1)
    b_c = bm.astype(f32).reshape(bsz, nc, B_CHUNK, B_GROUPS, B_STATE)
    c_c = cm.astype(f32).reshape(bsz, nc, B_CHUNK, B_GROUPS, B_STATE)
    a_cs = jnp.cumsum(ad, axis=-1)
    lmat = jnp.exp(segsum(ad))
    cb = jnp.einsum("bclgn,bcsgn->bcgls", c_c, b_c)
    y_diag = jnp.einsum("bcgjls,bcsgjp->bclgjp", cb[:, :, :, None] * lmat, xdt)
    decay_states = jnp.exp(a_cs[..., -1:] - a_cs)
    chunk_states = jnp.einsum("bclgn,bcgjl,bclgjp->bcgjpn", b_c, decay_states, xdt)
    chunk_decay = jnp.exp(a_cs[..., -1])

    def step(h, inp):
        st, dc = inp
        return dc[..., None, None] * h + st, h

    h0 = jnp.zeros((bsz, B_GROUPS, hpg, B_HEAD_DIM, B_STATE), f32)
    _, prev = lax.scan(step, h0, (jnp.moveaxis(chunk_states, 1, 0), jnp.moveaxis(chunk_decay, 1, 0)))
    prev = jnp.moveaxis(prev, 0, 1)
    y_off = jnp.einsum("bclgn,bcgjpn,bcgjl->bclgjp", c_c, prev, jnp.exp(a_cs))
    y = y_diag + y_off + x_c * d_skip.astype(f32).reshape(B_GROUPS, hpg)[:, :, None]
    y = y.reshape(bsz, s, B_WIDTH)
    y = rms_norm(y * jax.nn.silu(z.astype(f32)), norm_g)
    return y.astype(zxbcdt.dtype)


def chunk_sgu(uv, ln_g, ln_b, w_s, b_s):
    bsz, s, _ = uv.shape
    nc = s // C_CHUNK
    u, v = jnp.split(jax.nn.gelu(uv), 2, axis=-1)
    v = layer_norm(v, ln_g, ln_b)
    vc = v.reshape(bsz, nc, C_CHUNK, C_GROUPS, C_GROUP_DIM)
    mask = jnp.tril(jnp.ones((C_CHUNK, C_CHUNK), dtype=bool))
    w = jnp.where(mask, w_s, jnp.zeros_like(w_s))
    mixed = jnp.einsum("gts,bcsgd->bctgd", w, vc) + jnp.swapaxes(b_s, 0, 1)[:, :, None]
    return u * mixed.reshape(bsz, s, C_WIDTH)


def hybrid_mixer(h, w_in, w_out, lru_conv_w, lru_conv_b, lru_w_r, lru_b_r, lru_w_i, lru_b_i, lru_lambda,
                 ssd_conv_w, ssd_conv_b, ssd_dt_bias, ssd_a_log, ssd_d, ssd_norm_g,
                 sgu_ln_g, sgu_ln_b, sgu_w_s, sgu_b_s):
    proj = h @ w_in
    pa, pb, pc = jnp.split(proj, [2 * A_WIDTH, 2 * A_WIDTH + B_IN], axis=-1)
    gate_a, rec_a = jnp.split(pa, 2, axis=-1)
    rec = causal_dwconv(rec_a, lru_conv_w, lru_conv_b)
    ya = rg_lru(rec, lru_w_r, lru_b_r, lru_w_i, lru_b_i, lru_lambda) * jax.nn.gelu(gate_a)
    yb = ssd_mixer(pb, ssd_conv_w, ssd_conv_b, ssd_dt_bias, ssd_a_log, ssd_d, ssd_norm_g)
    yc = chunk_sgu(pc, sgu_ln_g, sgu_ln_b, sgu_w_s, sgu_b_s)
    return jnp.concatenate([ya, yb, yc], axis=-1) @ w_out


def setup_inputs(seed: int = 0) -> dict:
    key = jax.random.key(seed)
    ks = iter(jax.random.split(key, 40))
    f32 = jnp.float32
    L = DEPTH

    def nrm(shape, scale):
        return jax.random.normal(next(ks), shape, f32) * scale

    def gain(shape):
        return 1.0 + 0.05 * jax.random.normal(next(ks), shape, f32)

    x = jax.random.normal(next(ks), (BATCH, SEQ, D_MODEL), f32)
    ffn1_pre_g = gain((L, D_MODEL))
    ffn1_post_g = gain((L, D_MODEL))
    ffn1_w_gu = nrm((L, D_MODEL, 2 * D_FF), D_MODEL ** -0.5)
    ffn1_w_down = nrm((L, D_FF, D_MODEL), D_FF ** -0.5)
    mix_pre_g = gain((L, D_MODEL))
    mix_post_g = gain((L, D_MODEL))
    mix_w_in = nrm((L, D_MODEL, IN_COLS), D_MODEL ** -0.5)
    mix_w_out = nrm((L, D_MIX, D_MODEL), D_MIX ** -0.5)
    lru_conv_w = nrm((L, CONV_W, A_WIDTH), CONV_W ** -0.5)
    lru_conv_b = nrm((L, A_WIDTH), 0.02)
    lru_w_r = nrm((L, A_HEADS, A_HEAD_DIM, A_HEAD_DIM), A_HEAD_DIM ** -0.5)
    lru_b_r = nrm((L, A_WIDTH), 0.02)
    lru_w_i = nrm((L, A_HEADS, A_HEAD_DIM, A_HEAD_DIM), A_HEAD_DIM ** -0.5)
    lru_b_i = nrm((L, A_WIDTH), 0.02)
    a_c = jax.random.uniform(next(ks), (L, A_WIDTH), f32, 0.9, 0.999)
    a0 = a_c ** (1.0 / LRU_C)
    lru_lambda = jnp.log(a0) - jnp.log1p(-a0)
    ssd_conv_w = nrm((L, CONV_W, B_CONV_DIM), CONV_W ** -0.5)
    ssd_conv_b = nrm((L, B_CONV_DIM), 0.02)
    dt0 = jnp.exp(jax.random.uniform(next(ks), (L, B_HEADS), f32, math.log(1e-3), math.log(1e-1)))
    ssd_dt_bias = dt0 + jnp.log(-jnp.expm1(-dt0))
    ssd_a_log = jnp.log(jax.random.uniform(next(ks), (L, B_HEADS), f32, 1.0, 16.0))
    ssd_d = 1.0 + 0.1 * jax.random.normal(next(ks), (L, B_HEADS), f32)
    ssd_norm_g = gain((L, B_WIDTH))
    sgu_ln_g = gain((L, C_WIDTH))
    sgu_ln_b = nrm((L, C_WIDTH), 0.02)
    sgu_w_s = nrm((L, C_GROUPS, C_CHUNK, C_CHUNK), C_CHUNK ** -0.5)
    sgu_b_s = 1.0 + 0.1 * jax.random.normal(next(ks), (L, C_GROUPS, C_CHUNK), f32)
    ffn2_pre_g = gain((L, D_MODEL))
    ffn2_post_g = gain((L, D_MODEL))
    ffn2_w_gu = nrm((L, D_MODEL, 2 * D_FF), D_MODEL ** -0.5)
    ffn2_w_down = nrm((L, D_FF, D_MODEL), D_FF ** -0.5)
    return {
        "x": x,
        "ffn1_pre_g": ffn1_pre_g, "ffn1_post_g": ffn1_post_g,
        "ffn1_w_gu": ffn1_w_gu, "ffn1_w_down": ffn1_w_down,
        "mix_pre_g": mix_pre_g, "mix_post_g": mix_post_g,
        "mix_w_in": mix_w_in, "mix_w_out": mix_w_out,
        "lru_conv_w": lru_conv_w, "lru_conv_b": lru_conv_b,
        "lru_w_r": lru_w_r, "lru_b_r": lru_b_r,
        "lru_w_i": lru_w_i, "lru_b_i": lru_b_i, "lru_lambda": lru_lambda,
        "ssd_conv_w": ssd_conv_w, "ssd_conv_b": ssd_conv_b,
        "ssd_dt_bias": ssd_dt_bias, "ssd_a_log": ssd_a_log,
        "ssd_d": ssd_d, "ssd_norm_g": ssd_norm_g,
        "sgu_ln_g": sgu_ln_g, "sgu_ln_b": sgu_ln_b,
        "sgu_w_s": sgu_w_s, "sgu_b_s": sgu_b_s,
        "ffn2_pre_g": ffn2_pre_g, "ffn2_post_g": ffn2_post_g,
        "ffn2_w_gu": ffn2_w_gu, "ffn2_w_down": ffn2_w_down,
    }


def reference(x, ffn1_pre_g, ffn1_post_g, ffn1_w_gu, ffn1_w_down,
              mix_pre_g, mix_post_g, mix_w_in, mix_w_out,
              lru_conv_w, lru_conv_b, lru_w_r, lru_b_r, lru_w_i, lru_b_i, lru_lambda,
              ssd_conv_w, ssd_conv_b, ssd_dt_bias, ssd_a_log, ssd_d, ssd_norm_g,
              sgu_ln_g, sgu_ln_b, sgu_w_s, sgu_b_s,
              ffn2_pre_g, ffn2_post_g, ffn2_w_gu, ffn2_w_down):
    for l in range(DEPTH):
        f = swiglu(rms_norm(x, ffn1_pre_g[l]), ffn1_w_gu[l], ffn1_w_down[l])
        x = x + 0.5 * rms_norm(f, ffn1_post_g[l])
        m = hybrid_mixer(rms_norm(x, mix_pre_g[l]), mix_w_in[l], mix_w_out[l],
                         lru_conv_w[l], lru_conv_b[l], lru_w_r[l], lru_b_r[l],
                         lru_w_i[l], lru_b_i[l], lru_lambda[l],
                         ssd_conv_w[l], ssd_conv_b[l], ssd_dt_bias[l], ssd_a_log[l],
                         ssd_d[l], ssd_norm_g[l],
                         sgu_ln_g[l], sgu_ln_b[l], sgu_w_s[l], sgu_b_s[l])
        x = x + rms_norm(m, mix_post_g[l])
        f = swiglu(rms_norm(x, ffn2_pre_g[l]), ffn2_w_gu[l], ffn2_w_down[l])
        x = x + 0.5 * rms_norm(f, ffn2_post_g[l])
    return x
```

```python
import functools
import math

import jax
import jax.numpy as jnp
from jax import lax
from jax.experimental import pallas as pl
from jax.experimental.pallas import tpu as pltpu

F32 = jnp.float32
BF16 = jnp.bfloat16

NORM_EPS = 1e-6
CONV_W = 4
D_FF = 2816
A_WIDTH = 384
A_HEADS = 6
LRU_C = 8.0
B_WIDTH = 384
B_HEADS = 6
B_HEAD_DIM = 64
B_GROUPS = 2
B_STATE = 128
CHUNK = 128
C_WIDTH = 256
C_GROUPS = 4
C_GROUP_DIM = 64

LANES = 128
SUBLANES = 8

COL_GATE = 0
COL_Z = COL_GATE + A_WIDTH
COL_U = COL_Z + B_WIDTH
COL_V = COL_U + C_WIDTH
COL_CONV = COL_V + C_WIDTH
CONV_COLS = A_WIDTH + B_WIDTH + 2 * B_GROUPS * B_STATE
COL_DT = COL_CONV + CONV_COLS
IN_COLS_P = COL_DT + B_WIDTH

FFN_TOKENS = 512
FFN_CHUNK = 256
MIX_TIME = 512
VMEM_LIMIT = 56 * 1024 * 1024


def _rms(x, g):
    return x * lax.rsqrt(jnp.mean(x * x, axis=-1, keepdims=True) + NORM_EPS) * g


def _sigmoid(x):
    return 1.0 / (1.0 + jnp.exp(-x))


def _silu(x):
    return x * _sigmoid(x)


def _gelu_tanh(x):
    return x * (0.5 * (1.0 + jnp.tanh(math.sqrt(2.0 / math.pi) * (x + 0.044715 * (x * x * x)))))


def _softplus(x):
    return jnp.maximum(x, 0.0) + jnp.log1p(jnp.exp(-jnp.abs(x)))


def _dot(a, b):
    return jnp.dot(a, b, preferred_element_type=F32)


def _dot_nt(a, b):
    return lax.dot_general(a, b, (((1,), (1,)), ((), ())), preferred_element_type=F32)


def _dot_tn(a, b):
    return lax.dot_general(a, b, (((0,), (0,)), ((), ())), preferred_element_type=F32)


def _ffn_kernel(x_ref, pre_ref, post_ref, wgu_ref, wd_ref, o_ref, h_ref):
    x = x_ref[...]
    xn = _rms(x, pre_ref[...]).astype(BF16)
    for c in range(D_FF // FFN_CHUNK):
        lo = c * FFN_CHUNK
        g = _dot(xn, wgu_ref[:, lo:lo + FFN_CHUNK])
        u = _dot(xn, wgu_ref[:, D_FF + lo:D_FF + lo + FFN_CHUNK])
        h_ref[:, lo:lo + FFN_CHUNK] = (_silu(g) * u).astype(BF16)
    f = _dot(h_ref[...], wd_ref[...])
    o_ref[...] = x + 0.5 * _rms(f, post_ref[...])


def _ffn_call(x2, pre_g, post_g, w_gu, w_down, layer):
    t, d = x2.shape
    tm = min(FFN_TOKENS, t)
    assert t % tm == 0
    vec = pl.BlockSpec((None, 1, d), lambda i: (layer, 0, 0))
    return pl.pallas_call(
        _ffn_kernel,
        grid=(t // tm,),
        in_specs=[
            pl.BlockSpec((tm, d), lambda i: (i, 0)),
            vec,
            vec,
            pl.BlockSpec((None, d, 2 * D_FF), lambda i: (layer, 0, 0), pipeline_mode=pl.Buffered(1)),
            pl.BlockSpec((None, D_FF, d), lambda i: (layer, 0, 0), pipeline_mode=pl.Buffered(1)),
        ],
        out_specs=pl.BlockSpec((tm, d), lambda i: (i, 0)),
        out_shape=jax.ShapeDtypeStruct((t, d), F32),
        scratch_shapes=[pltpu.VMEM((tm, D_FF), BF16)],
        compiler_params=pltpu.CompilerParams(
            dimension_semantics=("parallel",), vmem_limit_bytes=VMEM_LIMIT),
        name="ffn",
    )(x2, pre_g, post_g, w_gu, w_down)


def _shift_rows(v, d, fill):
    n = v.shape[0]
    if d % SUBLANES == 0:
        return jnp.concatenate([jnp.full((d, v.shape[1]), fill, v.dtype), v[:n - d]], axis=0)
    rows = lax.broadcasted_iota(jnp.int32, v.shape, 0)
    return jnp.where(rows < d, fill, pltpu.roll(v, d, 0))


def _linear_scan(a, b):
    d = 1
    while d < a.shape[0]:
        b = a * _shift_rows(b, d, 0.0) + b
        a = a * _shift_rows(a, d, 1.0)
        d *= 2
    return a, b


def _split3(x):
    hi = x.astype(BF16)
    r = x - hi.astype(F32)
    mid = r.astype(BF16)
    lo = (r - mid.astype(F32)).astype(BF16)
    return hi, mid, lo


def _mixer_kernel(x_ref, pre_ref, post_ref, win_ref, wout_ref, convw_ref, convb_ref,
                  wri_ref, bri_ref, lam_ref, dtb_ref, alog_ref, dskip_ref, normg_ref,
                  lng_ref, lnb_ref, ws_ref, bs_ref,
                  o_ref,
                  pa_ref, cin_ref, dt_ref, ycat_ref, hl_ref, st_ref):
    ts = x_ref.shape[0]
    n_chunks = ts // CHUNK

    @pl.when(pl.program_id(1) == 0)
    def _():
        cin_ref[0:SUBLANES, :] = jnp.zeros((SUBLANES, CONV_COLS), F32)
        hl_ref[...] = jnp.zeros_like(hl_ref)
        st_ref[...] = jnp.zeros_like(st_ref)

    x = x_ref[...]
    xn = _rms(x, pre_ref[...]).astype(BF16)
    pa_ref[...] = _dot(xn, win_ref[:, 0:COL_CONV])
    cin_ref[SUBLANES:SUBLANES + ts, :] = _dot(xn, win_ref[:, COL_CONV:COL_DT])
    dt_ref[...] = _dot(xn, win_ref[:, COL_DT:IN_COLS_P])

    base = SUBLANES - (CONV_W - 1)
    conv = convb_ref[...] + convw_ref[0:1, :] * cin_ref[base:base + ts, :]
    for k in range(1, CONV_W):
        conv = conv + convw_ref[k:k + 1, :] * cin_ref[base + k:base + k + ts, :]
    cin_ref[0:SUBLANES, :] = cin_ref[ts:ts + SUBLANES, :]

    rec = conv[:, 0:A_WIDTH]
    ri = _dot(rec.astype(BF16), wri_ref[...]) + bri_ref[...]
    r = _sigmoid(ri[:, 0:A_WIDTH])
    gi = _sigmoid(ri[:, A_WIDTH:2 * A_WIDTH])
    log_a = (-LRU_C) * r * _softplus(-lam_ref[...])
    a = jnp.exp(log_a)
    mult = jnp.sqrt(-jnp.tanh(log_a) * (a * a + 1.0))
    a_cum, h = _linear_scan(a, mult * (gi * rec))
    h = h + a_cum * hl_ref[0:1, :]
    hl_ref[0:1, :] = h[ts - 1:ts, :]
    ycat_ref[:, 0:A_WIDTH] = (h * _gelu_tanh(pa_ref[:, COL_GATE:COL_GATE + A_WIDTH])).astype(BF16)

    xbc = _silu(conv[:, A_WIDTH:CONV_COLS])
    xs = xbc[:, 0:B_WIDTH]
    bm = xbc[:, B_WIDTH:B_WIDTH + B_GROUPS * B_STATE].astype(BF16)
    cm = xbc[:, B_WIDTH + B_GROUPS * B_STATE:].astype(BF16)
    dt = _softplus(dt_ref[...] + dtb_ref[...])
    ad = dt * (-jnp.exp(alog_ref[...]))
    xdt = xs * dt

    row = lax.broadcasted_iota(jnp.int32, (CHUNK, CHUNK), 0)
    col = lax.broadcasted_iota(jnp.int32, (CHUNK, CHUNK), 1)
    causal = row >= col
    tri = causal.astype(BF16)
    lane_lo = col < B_HEAD_DIM
    grp_lane = lax.broadcasted_iota(jnp.int32, (CHUNK, B_WIDTH), 1) < (B_WIDTH // B_GROUPS)
    grp_row = lax.broadcasted_iota(jnp.int32, (B_WIDTH, B_STATE), 0) < (B_WIDTH // B_GROUPS)

    y_parts = []
    for c in range(n_chunks):
        lo = c * CHUNK
        ad_c = ad[lo:lo + CHUNK, :]
        hi_, mid_, lo_ = _split3(ad_c)
        cs = _dot(tri, hi_) + _dot(tri, mid_) + _dot(tri, lo_)
        cs_t = jnp.transpose(cs)
        cs_last = cs[CHUNK - 1:CHUNK, :]
        xdt_c = xdt[lo:lo + CHUNK, :]
        xdt_b = xdt_c.astype(BF16)
        b_c = bm[lo:lo + CHUNK, :]
        c_c = cm[lo:lo + CHUNK, :]
        st_b = st_ref[...].astype(BF16)

        cb = [_dot_nt(c_c[:, g * B_STATE:(g + 1) * B_STATE], b_c[:, g * B_STATE:(g + 1) * B_STATE])
              for g in range(B_GROUPS)]
        y_tiles = []
        for k in range(B_WIDTH // LANES):
            pair = []
            for j in (2 * k, 2 * k + 1):
                seg = jnp.where(causal, cs[:, j * B_HEAD_DIM:j * B_HEAD_DIM + 1]
                                - cs_t[j * B_HEAD_DIM:j * B_HEAD_DIM + 1, :], -jnp.inf)
                m = (cb[j // (B_HEADS // B_GROUPS)] * jnp.exp(seg)).astype(BF16)
                pair.append(_dot(m, xdt_b[:, k * LANES:(k + 1) * LANES]))
            y_tiles.append(jnp.where(lane_lo, pair[0], pair[1]))
        y_diag = jnp.concatenate(y_tiles, axis=1)

        off = [_dot_nt(c_c[:, g * B_STATE:(g + 1) * B_STATE], st_b) for g in range(B_GROUPS)]
        y_off = jnp.exp(cs) * jnp.where(grp_lane, off[0], off[1])
        y_parts.append(y_diag + y_off + xs[lo:lo + CHUNK, :] * dskip_ref[...])

        xdec = (xdt_c * jnp.exp(cs_last - cs)).astype(BF16)
        s_all = _dot_tn(xdec, b_c)
        st_ref[...] = (jnp.exp(cs_t[:, CHUNK - 1:CHUNK]) * st_ref[...]
                       + jnp.where(grp_row, s_all[:, 0:B_STATE], s_all[:, B_STATE:2 * B_STATE]))

    y = jnp.concatenate(y_parts, axis=0)
    y = y * _silu(pa_ref[:, COL_Z:COL_Z + B_WIDTH])
    ycat_ref[:, A_WIDTH:A_WIDTH + B_WIDTH] = _rms(y, normg_ref[...]).astype(BF16)

    u = _gelu_tanh(pa_ref[:, COL_U:COL_U + C_WIDTH])
    v = _gelu_tanh(pa_ref[:, COL_V:COL_V + C_WIDTH])
    mu = jnp.mean(v, axis=-1, keepdims=True)
    vc = v - mu
    v = vc * lax.rsqrt(jnp.mean(vc * vc, axis=-1, keepdims=True) + NORM_EPS) * lng_ref[...] + lnb_ref[...]
    v_b = v.astype(BF16)
    grp_of_lane = lax.broadcasted_iota(jnp.int32, (CHUNK, C_WIDTH), 1) // C_GROUP_DIM
    w_tril = [jnp.where(causal, ws_ref[g], 0.0).astype(BF16) for g in range(C_GROUPS)]
    for c in range(n_chunks):
        lo = c * CHUNK
        v_c = v_b[lo:lo + CHUNK, :]
        mixed = bs_ref[...]
        for g in range(C_GROUPS):
            mixed = mixed + jnp.where(grp_of_lane == g, _dot(w_tril[g], v_c), 0.0)
        ycat_ref[lo:lo + CHUNK, A_WIDTH + B_WIDTH:] = (u[lo:lo + CHUNK, :] * mixed).astype(BF16)

    m = _dot(ycat_ref[...], wout_ref[...])
    o_ref[...] = x + _rms(m, post_ref[...])


def _mixer_call(x3, p, layer):
    b, s, d = x3.shape
    ts = min(MIX_TIME, s)
    assert s % ts == 0 and ts % CHUNK == 0

    def whole(arr, single=False):
        shape = arr.shape[1:]
        nd = len(shape)
        kw = {"pipeline_mode": pl.Buffered(1)} if single else {}
        return pl.BlockSpec((None,) + shape, lambda bi, ti: (layer,) + (0,) * nd, **kw)

    names = ["pre_g", "post_g", "w_in", "w_out", "conv_w", "conv_b", "w_ri", "b_ri", "lam", "dt_b",
             "a_log", "d_skip", "norm_g", "ln_g", "ln_b", "w_s", "b_s"]
    big = {"w_in", "w_out"}
    return pl.pallas_call(
        _mixer_kernel,
        grid=(b, s // ts),
        in_specs=[pl.BlockSpec((None, ts, d), lambda bi, ti: (bi, ti, 0))]
        + [whole(p[n], n in big) for n in names],
        out_specs=pl.BlockSpec((None, ts, d), lambda bi, ti: (bi, ti, 0)),
        out_shape=jax.ShapeDtypeStruct((b, s, d), F32),
        scratch_shapes=[
            pltpu.VMEM((ts, COL_CONV), F32),
            pltpu.VMEM((ts + SUBLANES, CONV_COLS), F32),
            pltpu.VMEM((ts, B_WIDTH), F32),
            pltpu.VMEM((ts, d), BF16),
            pltpu.VMEM((SUBLANES, A_WIDTH), F32),
            pltpu.VMEM((B_WIDTH, B_STATE), F32),
        ],
        compiler_params=pltpu.CompilerParams(
            dimension_semantics=("parallel", "arbitrary"), vmem_limit_bytes=VMEM_LIMIT),
        name="mixer",
    )(x3, *[p[n] for n in names])


def _block_diag(w):
    l, h, i, j = w.shape
    eye = jnp.eye(h, dtype=w.dtype)
    return jnp.einsum("lhij,hk->lhikj", w, eye).reshape(l, h * i, h * j)


def _mixer_params(mix_pre_g, mix_post_g, mix_w_in, mix_w_out, lru_conv_w, lru_conv_b, lru_w_r, lru_b_r,
                  lru_w_i, lru_b_i, lru_lambda, ssd_conv_w, ssd_conv_b, ssd_dt_bias, ssd_a_log, ssd_d,
                  ssd_norm_g, sgu_ln_g, sgu_ln_b, sgu_w_s, sgu_b_s):
    w = mix_w_in
    o_b = 2 * A_WIDTH
    o_xbc = o_b + B_WIDTH
    o_dt = o_xbc + (CONV_COLS - A_WIDTH)
    o_c = o_dt + B_HEADS
    w_in = jnp.concatenate([
        w[:, :, 0:A_WIDTH],
        w[:, :, o_b:o_xbc],
        w[:, :, o_c:o_c + 2 * C_WIDTH],
        w[:, :, A_WIDTH:2 * A_WIDTH],
        w[:, :, o_xbc:o_dt],
        jnp.repeat(w[:, :, o_dt:o_c], B_HEAD_DIM, axis=2),
    ], axis=2).astype(BF16)
    row = lambda v: v[:, None, :]
    rep = lambda v: jnp.repeat(v, B_HEAD_DIM, axis=1)[:, None, :]
    return {
        "pre_g": row(mix_pre_g), "post_g": row(mix_post_g),
        "w_in": w_in, "w_out": mix_w_out.astype(BF16),
        "conv_w": jnp.concatenate([lru_conv_w, ssd_conv_w], axis=2),
        "conv_b": row(jnp.concatenate([lru_conv_b, ssd_conv_b], axis=1)),
        "w_ri": jnp.concatenate([_block_diag(lru_w_r), _block_diag(lru_w_i)], axis=2).astype(BF16),
        "b_ri": row(jnp.concatenate([lru_b_r, lru_b_i], axis=1)),
        "lam": row(lru_lambda),
        "dt_b": rep(ssd_dt_bias), "a_log": rep(ssd_a_log), "d_skip": rep(ssd_d),
        "norm_g": row(ssd_norm_g), "ln_g": row(sgu_ln_g), "ln_b": row(sgu_ln_b),
        "w_s": sgu_w_s,
        "b_s": jnp.repeat(jnp.swapaxes(sgu_b_s, 1, 2), C_GROUP_DIM, axis=2),
    }


def kernel(x, ffn1_pre_g, ffn1_post_g, ffn1_w_gu, ffn1_w_down, mix_pre_g, mix_post_g, mix_w_in, mix_w_out, lru_conv_w, lru_conv_b, lru_w_r, lru_b_r, lru_w_i, lru_b_i, lru_lambda, ssd_conv_w, ssd_conv_b, ssd_dt_bias, ssd_a_log, ssd_d, ssd_norm_g, sgu_ln_g, sgu_ln_b, sgu_w_s, sgu_b_s, ffn2_pre_g, ffn2_post_g, ffn2_w_gu, ffn2_w_down):
    b, s, d = x.shape
    depth = ffn1_w_gu.shape[0]
    row = lambda v: v[:, None, :]
    f1 = (row(ffn1_pre_g), row(ffn1_post_g), ffn1_w_gu.astype(BF16), ffn1_w_down.astype(BF16))
    f2 = (row(ffn2_pre_g), row(ffn2_post_g), ffn2_w_gu.astype(BF16), ffn2_w_down.astype(BF16))
    mp = _mixer_params(mix_pre_g, mix_post_g, mix_w_in, mix_w_out, lru_conv_w, lru_conv_b, lru_w_r,
                       lru_b_r, lru_w_i, lru_b_i, lru_lambda, ssd_conv_w, ssd_conv_b, ssd_dt_bias,
                       ssd_a_log, ssd_d, ssd_norm_g, sgu_ln_g, sgu_ln_b, sgu_w_s, sgu_b_s)
    for l in range(depth):
        x = _ffn_call(x.reshape(b * s, d), *f1, l).reshape(b, s, d)
        x = _mixer_call(x, mp, l)
        x = _ffn_call(x.reshape(b * s, d), *f2, l).reshape(b, s, d)
    return x
```

```python
import functools
import math

import jax
import jax.numpy as jnp
from jax import lax
from jax.experimental import pallas as pl
from jax.experimental.pallas import tpu as pltpu

F32 = jnp.float32
BF16 = jnp.bfloat16

NORM_EPS = 1e-6
CONV_W = 4
D_FF = 2816
A_WIDTH = 384
A_HEADS = 6
LRU_C = 8.0
B_WIDTH = 384
B_HEADS = 6
B_HEAD_DIM = 64
B_GROUPS = 2
B_STATE = 128
CHUNK = 128
C_WIDTH = 256
C_GROUPS = 4
C_GROUP_DIM = 64

LANES = 128
SUBLANES = 8

SEG = CHUNK // SUBLANES
TAIL = (CONV_W - 1) * SUBLANES

COL_GATE = 0
COL_Z = COL_GATE + A_WIDTH
COL_U = COL_Z + B_WIDTH
COL_V = COL_U + C_WIDTH
COL_CONV = COL_V + C_WIDTH
CONV_COLS = A_WIDTH + B_WIDTH + 2 * B_GROUPS * B_STATE
COL_DT = COL_CONV + CONV_COLS
IN_COLS_P = COL_DT + B_WIDTH

FFN_TOKENS = 1024
FFN_SUB = 512
FFN_CHUNK = 256
MIX_TIME = 512
VMEM_LIMIT = 56 * 1024 * 1024

ROWS_NATURAL, ROWS_PERMUTE_OUT, ROWS_PERMUTED_IN = 0, 1, 2


def _rms(x, g):
    return x * lax.rsqrt(jnp.mean(x * x, axis=-1, keepdims=True) + NORM_EPS) * g


def _sigmoid(x):
    return 1.0 / (1.0 + jnp.exp(-x))


def _silu(x):
    return x * _sigmoid(x)


def _gelu_tanh(x):
    c = math.sqrt(2.0 / math.pi)
    hx = 0.5 * x
    return hx + hx * jnp.tanh(x * (c + (c * 0.044715) * (x * x)))


def _softplus(x):
    return jnp.maximum(x, 0.0) + jnp.log1p(jnp.exp(-jnp.abs(x)))


def _dot(a, b):
    return jnp.dot(a, b, preferred_element_type=F32)


def _dot_nt(a, b):
    return lax.dot_general(a, b, (((1,), (1,)), ((), ())), preferred_element_type=F32)


def _dot_tn(a, b):
    return lax.dot_general(a, b, (((0,), (0,)), ((), ())), preferred_element_type=F32)


def _ffn_kernel(x_ref, pre_ref, post_ref, wgu_ref, wd_ref, o_ref, h_ref, *slab_ref, row_mode):
    d = x_ref.shape[1]
    for s in range(x_ref.shape[0] // FFN_SUB):
        rows = slice(s * FFN_SUB, (s + 1) * FFN_SUB)
        x = x_ref[rows, :]
        xn = _rms(x, pre_ref[...]).astype(BF16)
        for c in range(D_FF // FFN_CHUNK):
            lo = c * FFN_CHUNK
            g = _dot(xn, wgu_ref[:, lo:lo + FFN_CHUNK])
            u = _dot(xn, wgu_ref[:, D_FF + lo:D_FF + lo + FFN_CHUNK])
            h_ref[rows, lo:lo + FFN_CHUNK] = (_silu(g) * u).astype(BF16)
        f = _dot(h_ref[rows, :], wd_ref[...])
        res = x + 0.5 * _rms(f, post_ref[...])
        if row_mode == ROWS_NATURAL:
            o_ref[rows, :] = res
            continue
        (slab,) = slab_ref
        for k in range(d // LANES):
            cols = slice(k * LANES, (k + 1) * LANES)
            if row_mode == ROWS_PERMUTE_OUT:
                for c in range(FFN_SUB // CHUNK):
                    for r in range(SUBLANES):
                        nat = c * CHUNK + r * SEG
                        slab[k, pl.ds(c * CHUNK + r, SEG, stride=SUBLANES), :] = res[nat:nat + SEG, cols]
                o_ref[rows, cols] = slab[k]
            else:
                slab[k] = res[:, cols]
                for c in range(FFN_SUB // CHUNK):
                    for r in range(SUBLANES):
                        nat = s * FFN_SUB + c * CHUNK + r * SEG
                        o_ref[nat:nat + SEG, cols] = slab[k, pl.ds(c * CHUNK + r, SEG, stride=SUBLANES), :]


def _ffn_call(x2, pre_g, post_g, w_gu, w_down, layer, row_mode):
    t, d = x2.shape
    tm = min(FFN_TOKENS, t)
    assert t % tm == 0 and tm % FFN_SUB == 0
    vec = pl.BlockSpec((None, 1, d), lambda i: (layer, 0, 0))
    scratch = [pltpu.VMEM((tm, D_FF), BF16)]
    if row_mode != ROWS_NATURAL:
        scratch.append(pltpu.VMEM((d // LANES, FFN_SUB, LANES), F32))
    return pl.pallas_call(
        functools.partial(_ffn_kernel, row_mode=row_mode),
        grid=(t // tm,),
        in_specs=[
            pl.BlockSpec((tm, d), lambda i: (i, 0)),
            vec,
            vec,
            pl.BlockSpec((None, d, 2 * D_FF), lambda i: (layer, 0, 0), pipeline_mode=pl.Buffered(1)),
            pl.BlockSpec((None, D_FF, d), lambda i: (layer, 0, 0), pipeline_mode=pl.Buffered(1)),
        ],
        out_specs=pl.BlockSpec((tm, d), lambda i: (i, 0)),
        out_shape=jax.ShapeDtypeStruct((t, d), F32),
        scratch_shapes=scratch,
        compiler_params=pltpu.CompilerParams(
            dimension_semantics=("parallel",), vmem_limit_bytes=VMEM_LIMIT),
        name="ffn",
    )(x2, pre_g, post_g, w_gu, w_down)


def _perm_time(idx):
    return (idx & (SUBLANES - 1)) * SEG + lax.shift_right_logical(idx, 3)


def _roll1(v):
    return pltpu.roll(v, 1, 0)


def _tiles(v):
    return [v[q * SUBLANES:(q + 1) * SUBLANES, :] for q in range(v.shape[0] // SUBLANES)]


def _scan8(a, b):
    rows = lax.broadcasted_iota(jnp.int32, a.shape, 0)
    for d in (1, 2, 4):
        keep = rows >= d
        b = a * jnp.where(keep, pltpu.roll(b, d, 0), 0.0) + b
        a = a * jnp.where(keep, pltpu.roll(a, d, 0), 1.0)
    return b


def _cumsum_time(v):
    run = _tiles(v)
    for q in range(1, SEG):
        run[q] = run[q - 1] + run[q]
    tot = run[-1]
    rows = lax.broadcasted_iota(jnp.int32, tot.shape, 0)
    before = tot
    for d in (1, 2, 4):
        before = before + jnp.where(rows >= d, pltpu.roll(before, d, 0), 0.0)
    before = before - tot
    return jnp.concatenate([t + before for t in run], axis=0)


def _mixer_kernel(x_ref, pre_ref, post_ref, win_ref, wout_ref, convw_ref, convb_ref,
                  wri_ref, bri_ref, lam_ref, dtb_ref, alog_ref, dskip_ref, normg_ref,
                  lng_ref, lnb_ref, ws_ref, bs_ref,
                  o_ref,
                  proj_ref, ycat_ref, tail_ref, hl_ref, st_ref):
    ts = x_ref.shape[0]
    n_chunks = ts // CHUNK

    @pl.when(pl.program_id(1) == 0)
    def _():
        tail_ref[...] = jnp.zeros_like(tail_ref)
        hl_ref[...] = jnp.zeros_like(hl_ref)
        st_ref[...] = jnp.zeros_like(st_ref)

    row = lax.broadcasted_iota(jnp.int32, (CHUNK, CHUNK), 0)
    col = lax.broadcasted_iota(jnp.int32, (CHUNK, CHUNK), 1)
    causal = _perm_time(row) >= _perm_time(col)
    lane_lo = lax.broadcasted_iota(jnp.int32, (CHUNK, LANES), 1) < B_HEAD_DIM
    st_grp0 = lax.broadcasted_iota(jnp.int32, (B_WIDTH, B_STATE), 0) < (B_WIDTH // B_GROUPS)
    grp_of_lane = lax.broadcasted_iota(jnp.int32, (CHUNK, C_WIDTH), 1) // C_GROUP_DIM
    last_row = lax.broadcasted_iota(jnp.int32, (SUBLANES, CONV_COLS), 0) == SUBLANES - 1
    first_row = lax.broadcasted_iota(jnp.int32, (SUBLANES, A_WIDTH), 0) == 0

    neg_sp = (-LRU_C) * _softplus(-lam_ref[...])
    a_head = -jnp.exp(alog_ref[...])
    w_cat = jnp.concatenate([jnp.where(causal, ws_ref[g], 0.0).astype(BF16) for g in range(C_GROUPS)], axis=1)

    tail = tail_ref[...]
    hcar = hl_ref[...]

    def in_proj(c):
        xn = _rms(x_ref[c * CHUNK:(c + 1) * CHUNK, :], pre_ref[...]).astype(BF16)
        proj_ref[c % 2] = _dot(xn, win_ref[...])

    def out_proj(c):
        rows = slice(c * CHUNK, (c + 1) * CHUNK)
        o_ref[rows, :] = x_ref[rows, :] + _rms(_dot(ycat_ref[c % 2], wout_ref[...]), post_ref[...])

    in_proj(0)
    for c in range(n_chunks):
        if c + 1 < n_chunks:
            in_proj(c + 1)
        pa = proj_ref[c % 2, :, 0:COL_CONV]
        cv = proj_ref[c % 2, :, COL_CONV:COL_DT]
        dt_raw = proj_ref[c % 2, :, COL_DT:IN_COLS_P]

        wrap = [_roll1(jnp.where(last_row, tail[j * SUBLANES:(j + 1) * SUBLANES, :],
                                 cv[CHUNK - TAIL + j * SUBLANES:CHUNK - TAIL + (j + 1) * SUBLANES, :]))
                for j in range(CONV_W - 1)]
        ext = jnp.concatenate(wrap + [cv], axis=0)
        conv = convb_ref[...] + convw_ref[CONV_W - 1:CONV_W, :] * cv
        for k in range(CONV_W - 1):
            conv = conv + convw_ref[k:k + 1, :] * ext[k * SUBLANES:k * SUBLANES + CHUNK, :]
        tail = cv[CHUNK - TAIL:CHUNK, :]

        rec = conv[:, 0:A_WIDTH]
        rec_b = rec.astype(BF16)
        ri = [_dot(rec_b[:, k * LANES:(k + 1) * LANES], wri_ref[k]) + bri_ref[k] for k in range(A_WIDTH // LANES)]
        r = _sigmoid(jnp.concatenate([t[:, 0:LANES] for t in ri], axis=1))
        gi = _sigmoid(jnp.concatenate([t[:, LANES:2 * LANES] for t in ri], axis=1))
        log_a = r * neg_sp
        a = jnp.exp(log_a)
        b = jnp.sqrt(-jnp.tanh(log_a) * (a * a + 1.0)) * (gi * rec)
        a_run = _tiles(a)
        h_run = _tiles(b)
        for q in range(1, SEG):
            h_run[q] = a_run[q] * h_run[q - 1] + h_run[q]
            a_run[q] = a_run[q] * a_run[q - 1]
        seg_end = _scan8(a_run[-1], jnp.where(first_row, a_run[-1] * hcar + h_run[-1], h_run[-1]))
        seg_in = jnp.where(first_row, hcar, _roll1(seg_end))
        h = jnp.concatenate([h_q + a_q * seg_in for h_q, a_q in zip(h_run, a_run)], axis=0)
        hcar = jnp.broadcast_to(seg_end[SUBLANES - 1:SUBLANES, :], hcar.shape)
        ya = h * _gelu_tanh(pa[:, COL_GATE:COL_GATE + A_WIDTH])
        if c > 0:
            out_proj(c - 1)

        xbc = _silu(conv[:, A_WIDTH:CONV_COLS])
        xs = xbc[:, 0:B_WIDTH]
        b_c = xbc[:, B_WIDTH:B_WIDTH + B_GROUPS * B_STATE].astype(BF16)
        c_c = xbc[:, B_WIDTH + B_GROUPS * B_STATE:].astype(BF16)
        dt = _softplus(dt_raw + dtb_ref[...])
        xdt = xs * dt
        xdt_b = xdt.astype(BF16)
        cs = _cumsum_time(dt * a_head)
        cs_t = jnp.transpose(cs)
        cs_last = cs[CHUNK - 1:CHUNK, :]
        st_b = st_ref[...].astype(BF16)

        cb = [_dot_nt(c_c[:, g * B_STATE:(g + 1) * B_STATE], b_c[:, g * B_STATE:(g + 1) * B_STATE])
              for g in range(B_GROUPS)]
        y_tiles = []
        for k in range(B_WIDTH // LANES):
            ms = []
            for j in (2 * k, 2 * k + 1):
                seg = jnp.where(causal, cs[:, j * B_HEAD_DIM:j * B_HEAD_DIM + 1]
                                - cs_t[j * B_HEAD_DIM:j * B_HEAD_DIM + 1, :], -jnp.inf)
                ms.append((cb[j // (B_HEADS // B_GROUPS)] * jnp.exp(seg)).astype(BF16))
            x_k = xdt_b[:, k * LANES:(k + 1) * LANES]
            zero = jnp.zeros_like(x_k)
            y_tiles.append(_dot(jnp.concatenate(ms, axis=1),
                                jnp.concatenate([jnp.where(lane_lo, x_k, zero), jnp.where(lane_lo, zero, x_k)], axis=0)))
        y_diag = jnp.concatenate(y_tiles, axis=1)
        st_cat = jnp.concatenate([jnp.where(st_grp0, st_b, jnp.zeros_like(st_b)),
                                  jnp.where(st_grp0, jnp.zeros_like(st_b), st_b)], axis=1)
        y_off = jnp.exp(cs) * _dot_nt(c_c, st_cat)
        y = y_diag + y_off + xs * dskip_ref[...]

        xdec = (xdt * jnp.exp(cs_last - cs)).astype(BF16)
        s_all = _dot_tn(xdec, b_c)
        st_ref[...] = (jnp.exp(cs_t[:, CHUNK - 1:CHUNK]) * st_ref[...]
                       + jnp.where(st_grp0, s_all[:, 0:B_STATE], s_all[:, B_STATE:2 * B_STATE]))

        y = y * _silu(pa[:, COL_Z:COL_Z + B_WIDTH])
        yb = _rms(y, normg_ref[...])

        u = _gelu_tanh(pa[:, COL_U:COL_U + C_WIDTH])
        v = _gelu_tanh(pa[:, COL_V:COL_V + C_WIDTH])
        vc = v - jnp.mean(v, axis=-1, keepdims=True)
        v = vc * lax.rsqrt(jnp.mean(vc * vc, axis=-1, keepdims=True) + NORM_EPS) * lng_ref[...] + lnb_ref[...]
        v_b = v.astype(BF16)
        v_stack = jnp.concatenate([jnp.where(grp_of_lane == g, v_b, jnp.zeros_like(v_b)) for g in range(C_GROUPS)],
                                  axis=0)
        yc = u * (bs_ref[...] + _dot(w_cat, v_stack))

        ycat_ref[c % 2] = jnp.concatenate([ya, yb, yc], axis=1).astype(BF16)

    out_proj(n_chunks - 1)
    tail_ref[...] = tail
    hl_ref[...] = hcar


def _mixer_call(x3, p, layer):
    b, s, d = x3.shape
    ts = min(MIX_TIME, s)
    assert s % ts == 0 and ts % CHUNK == 0

    def whole(arr, single=False):
        shape = arr.shape[1:]
        nd = len(shape)
        kw = {"pipeline_mode": pl.Buffered(1)} if single else {}
        return pl.BlockSpec((None,) + shape, lambda bi, ti: (layer,) + (0,) * nd, **kw)

    names = ["pre_g", "post_g", "w_in", "w_out", "conv_w", "conv_b", "w_ri", "b_ri", "lam", "dt_b",
             "a_log", "d_skip", "norm_g", "ln_g", "ln_b", "w_s", "b_s"]
    big = {"w_in", "w_out"}
    return pl.pallas_call(
        _mixer_kernel,
        grid=(b, s // ts),
        in_specs=[pl.BlockSpec((None, ts, d), lambda bi, ti: (bi, ti, 0))]
        + [whole(p[n], n in big) for n in names],
        out_specs=pl.BlockSpec((None, ts, d), lambda bi, ti: (bi, ti, 0)),
        out_shape=jax.ShapeDtypeStruct((b, s, d), F32),
        scratch_shapes=[
            pltpu.VMEM((2, CHUNK, IN_COLS_P), F32),
            pltpu.VMEM((2, CHUNK, d), BF16),
            pltpu.VMEM((TAIL, CONV_COLS), F32),
            pltpu.VMEM((SUBLANES, A_WIDTH), F32),
            pltpu.VMEM((B_WIDTH, B_STATE), F32),
        ],
        compiler_params=pltpu.CompilerParams(
            dimension_semantics=("parallel", "arbitrary"), vmem_limit_bytes=VMEM_LIMIT),
        name="mixer",
    )(x3, *[p[n] for n in names])


def _gate_blocks(w_r, w_i):
    l, h, i, j = w_r.shape
    per_tile = LANES // i
    eye = jnp.eye(per_tile, dtype=w_r.dtype)

    def blocks(w):
        w = w.reshape(l, h // per_tile, per_tile, i, j)
        return jnp.einsum("ltaij,ab->ltaibj", w, eye).reshape(l, h // per_tile, LANES, LANES)

    return jnp.concatenate([blocks(w_r), blocks(w_i)], axis=3)


def _mixer_params(mix_pre_g, mix_post_g, mix_w_in, mix_w_out, lru_conv_w, lru_conv_b, lru_w_r, lru_b_r,
                  lru_w_i, lru_b_i, lru_lambda, ssd_conv_w, ssd_conv_b, ssd_dt_bias, ssd_a_log, ssd_d,
                  ssd_norm_g, sgu_ln_g, sgu_ln_b, sgu_w_s, sgu_b_s):
    w = mix_w_in
    o_b = 2 * A_WIDTH
    o_xbc = o_b + B_WIDTH
    o_dt = o_xbc + (CONV_COLS - A_WIDTH)
    o_c = o_dt + B_HEADS
    w_in = jnp.concatenate([
        w[:, :, 0:A_WIDTH],
        w[:, :, o_b:o_xbc],
        w[:, :, o_c:o_c + 2 * C_WIDTH],
        w[:, :, A_WIDTH:2 * A_WIDTH],
        w[:, :, o_xbc:o_dt],
        jnp.repeat(w[:, :, o_dt:o_c], B_HEAD_DIM, axis=2),
    ], axis=2).astype(BF16)
    row = lambda v: v[:, None, :]
    rep = lambda v: jnp.repeat(v, B_HEAD_DIM, axis=1)[:, None, :]
    idx = jnp.arange(CHUNK)
    tperm = (idx % SUBLANES) * SEG + idx // SUBLANES
    n_t = A_WIDTH // LANES
    depth = lru_b_r.shape[0]
    b_ri = jnp.concatenate([lru_b_r.reshape(depth, n_t, 1, LANES), lru_b_i.reshape(depth, n_t, 1, LANES)], axis=3)
    return {
        "pre_g": row(mix_pre_g), "post_g": row(mix_post_g),
        "w_in": w_in, "w_out": mix_w_out.astype(BF16),
        "conv_w": jnp.concatenate([lru_conv_w, ssd_conv_w], axis=2),
        "conv_b": row(jnp.concatenate([lru_conv_b, ssd_conv_b], axis=1)),
        "w_ri": _gate_blocks(lru_w_r, lru_w_i).astype(BF16), "b_ri": b_ri,
        "lam": row(lru_lambda),
        "dt_b": rep(ssd_dt_bias), "a_log": rep(ssd_a_log), "d_skip": rep(ssd_d),
        "norm_g": row(ssd_norm_g), "ln_g": row(sgu_ln_g), "ln_b": row(sgu_ln_b),
        "w_s": sgu_w_s[:, :, tperm][:, :, :, tperm],
        "b_s": jnp.repeat(jnp.swapaxes(sgu_b_s, 1, 2), C_GROUP_DIM, axis=2)[:, tperm],
    }


def kernel(x, ffn1_pre_g, ffn1_post_g, ffn1_w_gu, ffn1_w_down, mix_pre_g, mix_post_g, mix_w_in, mix_w_out, lru_conv_w, lru_conv_b, lru_w_r, lru_b_r, lru_w_i, lru_b_i, lru_lambda, ssd_conv_w, ssd_conv_b, ssd_dt_bias, ssd_a_log, ssd_d, ssd_norm_g, sgu_ln_g, sgu_ln_b, sgu_w_s, sgu_b_s, ffn2_pre_g, ffn2_post_g, ffn2_w_gu, ffn2_w_down):
    b, s, d = x.shape
    depth = ffn1_w_gu.shape[0]
    row = lambda v: v[:, None, :]
    f1 = (row(ffn1_pre_g), row(ffn1_post_g), ffn1_w_gu.astype(BF16), ffn1_w_down.astype(BF16))
    f2 = (row(ffn2_pre_g), row(ffn2_post_g), ffn2_w_gu.astype(BF16), ffn2_w_down.astype(BF16))
    mp = _mixer_params(mix_pre_g, mix_post_g, mix_w_in, mix_w_out, lru_conv_w, lru_conv_b, lru_w_r,
                       lru_b_r, lru_w_i, lru_b_i, lru_lambda, ssd_conv_w, ssd_conv_b, ssd_dt_bias,
                       ssd_a_log, ssd_d, ssd_norm_g, sgu_ln_g, sgu_ln_b, sgu_w_s, sgu_b_s)
    for l in range(depth):
        x = _ffn_call(x.reshape(b * s, d), *f1, l, ROWS_PERMUTE_OUT).reshape(b, s, d)
        x = _mixer_call(x, mp, l)
        x = _ffn_call(x.reshape(b * s, d), *f2, l, ROWS_PERMUTED_IN).reshape(b, s, d)
    return x
```
